```python
import math
import jax, jax.numpy as jnp
from jax import lax
import numpy as np

D_MODEL = 4096
BATCH = 2
SEQ = 8192
DEPTH = 4
DEC_BATCH = 1
DEC_SEQ = 8192
PAST_LEN = 128

HEAD_DIM = 128
N_HEADS = D_MODEL // HEAD_DIM
N_HEADS_A = N_HEADS // 2
N_HEADS_B = N_HEADS - N_HEADS_A
WIDTH_A = N_HEADS_A * HEAD_DIM
WIDTH_B = N_HEADS_B * HEAD_DIM
D_FF = 4 * D_MODEL
DILATED_PATTERNS = ((128, 1), (512, 4), (2048, 16))
N_BUCKETS = 32
MAX_DISTANCE = 2048
GRID_W = 64
NA_ROWS = 8
NA_COLS = 16
ALPHA = (2 * DEPTH) ** 0.25
BETA = (8 * DEPTH) ** -0.25
LN_EPS = 1e-5
RMS_EPS = 1e-6
NEG = -1e30
SCALE = HEAD_DIM ** -0.5

kernel_name = "hymba_dilated_natten_deepnorm_encoder"


def layer_norm(x, g, b):
    xf = x.astype(jnp.float32)
    mu = xf.mean(-1, keepdims=True)
    xc = xf - mu
    var = (xc * xc).mean(-1, keepdims=True)
    return (xc * lax.rsqrt(var + LN_EPS) * g.astype(jnp.float32) + b.astype(jnp.float32)).astype(x.dtype)


def branch_norm(o, g):
    return o * lax.rsqrt((o * o).mean(-1, keepdims=True) + RMS_EPS) * g.astype(jnp.float32)


def t5_bucket(rel):
    nb = N_BUCKETS // 2
    ret = jnp.where(rel > 0, nb, 0)
    n = jnp.abs(rel)
    max_exact = nb // 2
    n_f = jnp.maximum(n, 1).astype(jnp.float32)
    large = max_exact + (jnp.log(n_f / max_exact) / math.log(MAX_DISTANCE / max_exact)
                         * (nb - max_exact)).astype(jnp.int32)
    large = jnp.minimum(large, nb - 1)
    return ret + jnp.where(n < max_exact, n, large)


def dilated_branch(q, k, v, rel_bias, window, dil):
    B, T, H, E = q.shape
    half = window // (2 * dil)
    L = T // dil
    nb = -(-L // half)
    Lp = nb * half

    def strided(a):
        return a.reshape(B, L, dil, H, E)

    qs = jnp.pad(strided(q), ((0, 0), (0, Lp - L), (0, 0), (0, 0), (0, 0))).reshape(B, nb, half, dil, H, E)

    def band(a):
        ap = jnp.pad(strided(a), ((0, 0), (half, Lp - L + half), (0, 0), (0, 0), (0, 0)))
        ap = ap.reshape(B, nb + 2, half, dil, H, E)
        return jnp.concatenate([ap[:, :-2], ap[:, 1:-1], ap[:, 2:]], axis=2)

    kb = band(k)
    vb = band(v)
    i = jnp.arange(half)
    j = jnp.arange(3 * half)
    n = jnp.arange(nb)
    rel = j[None, :] - half - i[:, None]
    lk = (n[:, None] - 1) * half + j[None, :]
    valid = (jnp.abs(rel) <= half)[None] & ((lk >= 0) & (lk < L))[:, None, :]
    bias = jnp.transpose(rel_bias[t5_bucket(rel * dil)].astype(jnp.float32), (2, 0, 1))
    s = jnp.einsum('bnqrhe,bnkrhe->bnrhqk', qs, kb, preferred_element_type=jnp.float32) * SCALE + bias
    s = jnp.where(valid[None, :, None, None], s, NEG)
    m = s.max(-1, keepdims=True)
    p = jnp.exp(s - m)
    den = p.sum(-1)
    lse = m[..., 0] + jnp.log(den)
    o = jnp.einsum('bnrhqk,bnkrhe->bnqrhe', p, vb.astype(jnp.float32))
    den = jnp.transpose(den, (0, 1, 4, 2, 3))
    lse = jnp.transpose(lse, (0, 1, 4, 2, 3))
    o = o / den[..., None]
    o = o.reshape(B, Lp, dil, H, E)[:, :L].reshape(B, T, H, E)
    lse = lse.reshape(B, Lp, dil, H)[:, :L].reshape(B, T, H)
    return o, lse


def dilated_attention(q, k, v, rel_bias):
    outs, lses = [], []
    for window, dil in DILATED_PATTERNS:
        o, l = dilated_branch(q, k, v, rel_bias, window, dil)
        outs.append(o)
        lses.append(l)
    wts = jax.nn.softmax(jnp.stack(lses, 0), axis=0)
    return jnp.einsum('pbth,pbthe->bthe', wts, jnp.stack(outs, 0))


def neighbourhood_attention(q, k, v, rpb):
    B, T, H, E = q.shape
    rows = T // GRID_W
    wr = min(NA_ROWS, rows)
    r = jnp.arange(rows)
    rs = jnp.clip(r - wr // 2, 0, rows - wr)
    krow = rs[:, None] + jnp.arange(wr)
    c = jnp.arange(GRID_W)
    cs = jnp.clip(c - NA_COLS // 2, 0, GRID_W - NA_COLS)
    qg = q.reshape(B, rows, GRID_W, H, E)
    kg = k.reshape(B, rows, GRID_W, H, E)[:, krow].reshape(B, rows, wr * GRID_W, H, E)
    vg = v.reshape(B, rows, GRID_W, H, E)[:, krow].reshape(B, rows, wr * GRID_W, H, E)
    col_ok = (c[None, :] >= cs[:, None]) & (c[None, :] < cs[:, None] + NA_COLS)
    col_ok = jnp.broadcast_to(col_ok[:, None, :], (GRID_W, wr, GRID_W)).reshape(GRID_W, wr * GRID_W)
    dr = krow - r[:, None]
    dc = jnp.clip(c[None, :] - c[:, None], -(NA_COLS - 1), NA_COLS - 1)
    bias = rpb.astype(jnp.float32)[:, dr[:, None, :, None] + NA_ROWS - 1, dc[None, :, None, :] + NA_COLS - 1]
    bias = jnp.transpose(bias.reshape(H, rows, GRID_W, wr * GRID_W), (1, 0, 2, 3))
    s = jnp.einsum('brqhe,brkhe->brhqk', qg, kg, preferred_element_type=jnp.float32) * SCALE + bias
    s = jnp.where(col_ok[None, None, None], s, NEG)
    p = jax.nn.softmax(s, axis=-1)
    o = jnp.einsum('brhqk,brkhe->brqhe', p, vg.astype(jnp.float32))
    return o.reshape(B, T, H, E)


def encoder_layer(x, w_in, w_out, b_out, g_a, g_b, rel_bias, rpb,
                  ln1_g, ln1_b, w1, b1, w2, b2, ln2_g, ln2_b):
    B, T, _ = x.shape
    qkv = x @ w_in
    cuts = np.cumsum([WIDTH_A, WIDTH_A, WIDTH_A, WIDTH_B, WIDTH_B])
    qa, ka, va, qb, kb, vb = jnp.split(qkv, [int(cc) for cc in cuts], axis=-1)
    ha = lambda a: a.reshape(B, T, N_HEADS_A, HEAD_DIM)
    hb = lambda a: a.reshape(B, T, N_HEADS_B, HEAD_DIM)
    o_a = dilated_attention(ha(qa), ha(ka), ha(va), rel_bias).reshape(B, T, WIDTH_A)
    o_b = neighbourhood_attention(hb(qb), hb(kb), hb(vb), rpb).reshape(B, T, WIDTH_B)
    mix = jnp.concatenate([branch_norm(o_a, g_a), branch_norm(o_b, g_b)], axis=-1).astype(x.dtype)
    x = layer_norm(ALPHA * x + (mix @ w_out + b_out), ln1_g, ln1_b)
    h = jnp.square(jax.nn.relu(x @ w1 + b1))
    x = layer_norm(ALPHA * x + (h @ w2 + b2), ln2_g, ln2_b)
    return x


def trunk(x, w_in, w_out, b_out, g_a, g_b, rel_bias, rpb,
          ln1_g, ln1_b, w1, b1, w2, b2, ln2_g, ln2_b):
    for l in range(DEPTH):
        x = encoder_layer(x, w_in[l], w_out[l], b_out[l], g_a[l], g_b[l], rel_bias, rpb[l],
                          ln1_g[l], ln1_b[l], w1[l], b1[l], w2[l], b2[l], ln2_g[l], ln2_b[l])
    return x


def setup_inputs(seed: int = 0) -> dict:
    key = jax.random.key(seed)
    ks = jax.random.split(key, 20)
    f32 = jnp.float32
    nrm = lambda k, shape: jax.random.normal(k, shape, f32)
    col_scale = jnp.concatenate([
        jnp.ones((2 * WIDTH_A,), f32), jnp.full((WIDTH_A,), BETA, f32),
        jnp.ones((2 * WIDTH_B,), f32), jnp.full((WIDTH_B,), BETA, f32)])
    return {
        "x_prompt": nrm(ks[0], (BATCH, SEQ, D_MODEL)),
        "x_sample": nrm(ks[1], (DEC_BATCH, DEC_SEQ, D_MODEL)),
        "w_in": nrm(ks[2], (DEPTH, D_MODEL, 3 * D_MODEL)) * (D_MODEL ** -0.5) * col_scale,
        "w_out": nrm(ks[3], (DEPTH, D_MODEL, D_MODEL)) * (D_MODEL ** -0.5) * BETA,
        "b_out": nrm(ks[4], (DEPTH, D_MODEL)) * 0.01,
        "g_a": 1.0 + 0.01 * nrm(ks[5], (DEPTH, WIDTH_A)),
        "g_b": 1.0 + 0.01 * nrm(ks[6], (DEPTH, WIDTH_B)),
        "rel_bias": nrm(ks[7], (N_BUCKETS, N_HEADS_A)) * 0.5,
        "rpb": nrm(ks[8], (DEPTH, N_HEADS_B, 2 * NA_ROWS - 1, 2 * NA_COLS - 1)) * 0.5,
        "ln1_g": 1.0 + 0.01 * nrm(ks[9], (DEPTH, D_MODEL)),
        "ln1_b": 0.01 * nrm(ks[10], (DEPTH, D_MODEL)),
        "w1": nrm(ks[11], (DEPTH, D_MODEL, D_FF)) * (D_MODEL ** -0.5) * BETA,
        "b1": 0.01 * nrm(ks[12], (DEPTH, D_FF)),
        "w2": nrm(ks[13], (DEPTH, D_FF, D_MODEL)) * (D_FF ** -0.5) * BETA,
        "b2": 0.01 * nrm(ks[14], (DEPTH, D_MODEL)),
        "ln2_g": 1.0 + 0.01 * nrm(ks[15], (DEPTH, D_MODEL)),
        "ln2_b": 0.01 * nrm(ks[16], (DEPTH, D_MODEL)),
    }


def reference(x_prompt, x_sample, w_in, w_out, b_out, g_a, g_b, rel_bias, rpb,
              ln1_g, ln1_b, w1, b1, w2, b2, ln2_g, ln2_b):
    y_prompt = trunk(x_prompt, w_in, w_out, b_out, g_a, g_b, rel_bias, rpb,
                     ln1_g, ln1_b, w1, b1, w2, b2, ln2_g, ln2_b)
    y_sample = trunk(x_sample, w_in, w_out, b_out, g_a, g_b, rel_bias, rpb,
                     ln1_g, ln1_b, w1, b1, w2, b2, ln2_g, ln2_b)
    return (y_prompt, y_sample)
```

```python
import functools
import math

import jax
import jax.numpy as jnp
from jax import lax
from jax.experimental import pallas as pl
from jax.experimental.pallas import tpu as pltpu

F32 = jnp.float32
BF16 = jnp.bfloat16

HEAD_DIM = 128
DILATED_PATTERNS = ((128, 1), (512, 4), (2048, 16))
N_BUCKETS = 32
MAX_DISTANCE = 2048
GRID_W = 64
NA_ROWS = 8
NA_COLS = 16
LN_EPS = 1e-5
RMS_EPS = 1e-6
NEG = -1e30
SCALE = HEAD_DIM ** -0.5

LANES = 128
TQ = 128
HALF = 64
BAND = TQ + 2 * HALF
UNROLL_A = 4
UNROLL_B = 4
VMEM_LIMIT_BYTES = 56 * 1024 * 1024

_NT = (((1,), (1,)), ((), ()))


def _params(sem):
    return pltpu.CompilerParams(dimension_semantics=sem, vmem_limit_bytes=VMEM_LIMIT_BYTES)


def _mm_heads_kernel(x_ref, w_ref, o_ref):
    acc = jnp.dot(x_ref[...], w_ref[...], preferred_element_type=F32)
    for s in range(o_ref.shape[1]):
        o_ref[0, s] = acc[:, s * LANES:(s + 1) * LANES].astype(o_ref.dtype)


def _matmul_heads(x, w, n_seq, out_dtype, tm=1024, tn=1024):
    m, k = x.shape
    n = w.shape[1]
    tm, tn = min(tm, m), min(tn, n)
    t = m // n_seq
    tiles_per_seq = t // tm
    return pl.pallas_call(
        _mm_heads_kernel,
        grid=(m // tm, n // tn),
        in_specs=[pl.BlockSpec((tm, k), lambda i, j: (i, 0)),
                  pl.BlockSpec((k, tn), lambda i, j: (0, j))],
        out_specs=pl.BlockSpec((1, tn // LANES, tm, LANES),
                               lambda i, j: (i // tiles_per_seq, j, i % tiles_per_seq, 0)),
        out_shape=jax.ShapeDtypeStruct((n_seq, n // LANES, t, LANES), out_dtype),
        compiler_params=_params(("parallel", "arbitrary")),
    )(x, w)


def _mm_relu2_kernel(x_ref, w_ref, b_ref, o_ref):
    acc = jnp.dot(x_ref[...], w_ref[...], preferred_element_type=F32) + b_ref[...]
    o_ref[...] = jnp.square(jnp.maximum(acc, 0.0)).astype(o_ref.dtype)


def _matmul_relu2(x, w, b, tm=1024, tn=1024):
    m, k = x.shape
    n = w.shape[1]
    tm, tn = min(tm, m), min(tn, n)
    return pl.pallas_call(
        _mm_relu2_kernel,
        grid=(m // tm, n // tn),
        in_specs=[pl.BlockSpec((tm, k), lambda i, j: (i, 0)),
                  pl.BlockSpec((k, tn), lambda i, j: (0, j)),
                  pl.BlockSpec((1, tn), lambda i, j: (0, j))],
        out_specs=pl.BlockSpec((tm, tn), lambda i, j: (i, j)),
        out_shape=jax.ShapeDtypeStruct((m, n), BF16),
        compiler_params=_params(("parallel", "arbitrary")),
    )(x, w, b)


def _mm_resid_kernel(x_ref, w_ref, b_ref, r_ref, o_ref, acc_ref, *, nk, alpha):
    kk = pl.program_id(2)
    part = jnp.dot(x_ref[...], w_ref[...], preferred_element_type=F32)

    @pl.when(kk == 0)
    def _():
        acc_ref[...] = part

    @pl.when(kk > 0)
    def _():
        acc_ref[...] += part

    @pl.when(kk == nk - 1)
    def _():
        o_ref[...] = alpha * r_ref[...] + (acc_ref[...] + b_ref[...])


def _matmul_resid(x, w, b, resid, alpha, tm=1024, tn=1024, tk=2048):
    m, k = x.shape
    n = w.shape[1]
    tm, tn, tk = min(tm, m), min(tn, n), min(tk, k)
    nk = k // tk
    return pl.pallas_call(
        functools.partial(_mm_resid_kernel, nk=nk, alpha=alpha),
        grid=(m // tm, n // tn, nk),
        in_specs=[pl.BlockSpec((tm, tk), lambda i, j, kk: (i, kk)),
                  pl.BlockSpec((tk, tn), lambda i, j, kk: (kk, j)),
                  pl.BlockSpec((1, tn), lambda i, j, kk: (0, j)),
                  pl.BlockSpec((tm, tn), lambda i, j, kk: (i, j))],
        out_specs=pl.BlockSpec((tm, tn), lambda i, j, kk: (i, j)),
        out_shape=jax.ShapeDtypeStruct((m, n), F32),
        scratch_shapes=[pltpu.VMEM((tm, tn), F32)],
        compiler_params=_params(("parallel", "arbitrary", "arbitrary")),
    )(x, w, b, resid)


def _ln_kernel(z_ref, g_ref, b_ref, o32_ref, o16_ref):
    z = z_ref[...]
    mu = jnp.mean(z, axis=-1, keepdims=True)
    zc = z - mu
    var = jnp.mean(zc * zc, axis=-1, keepdims=True)
    y = zc * lax.rsqrt(var + LN_EPS) * g_ref[...] + b_ref[...]
    o32_ref[...] = y
    o16_ref[...] = y.astype(BF16)


def _layer_norm(z, g, b, tm=256):
    m, d = z.shape
    row = pl.BlockSpec((tm, d), lambda i: (i, 0))
    vec = pl.BlockSpec((1, d), lambda i: (0, 0))
    return pl.pallas_call(
        _ln_kernel,
        grid=(m // tm,),
        in_specs=[row, vec, vec],
        out_specs=[row, row],
        out_shape=[jax.ShapeDtypeStruct((m, d), F32), jax.ShapeDtypeStruct((m, d), BF16)],
        compiler_params=_params(("parallel",)),
    )(z, g, b)


def _mix_kernel(oa_ref, ob_ref, ga_ref, gb_ref, o_ref):
    col = 0
    for src, g_ref in ((oa_ref, ga_ref), (ob_ref, gb_ref)):
        nh = src.shape[1]
        ss = None
        for h in range(nh):
            v = src[0, h]
            part = jnp.sum(v * v, axis=-1, keepdims=True)
            ss = part if ss is None else ss + part
        inv = lax.rsqrt(ss / (nh * LANES) + RMS_EPS)
        for h in range(nh):
            g = g_ref[:, h * LANES:(h + 1) * LANES]
            o_ref[:, col:col + LANES] = (src[0, h] * inv * g).astype(o_ref.dtype)
            col += LANES


def _branch_norm_concat(o_a, o_b, g_a, g_b, tm=512):
    n_seq, ha, t, _ = o_a.shape
    hb = o_b.shape[1]
    tiles_per_seq = t // tm
    width = (ha + hb) * LANES

    def head_spec(nh):
        return pl.BlockSpec((1, nh, tm, LANES), lambda i: (i // tiles_per_seq, 0, i % tiles_per_seq, 0))

    return pl.pallas_call(
        _mix_kernel,
        grid=(n_seq * tiles_per_seq,),
        in_specs=[head_spec(ha), head_spec(hb),
                  pl.BlockSpec((1, ha * LANES), lambda i: (0, 0)),
                  pl.BlockSpec((1, hb * LANES), lambda i: (0, 0))],
        out_specs=pl.BlockSpec((tm, width), lambda i: (i, 0)),
        out_shape=jax.ShapeDtypeStruct((n_seq * t, width), BF16),
        compiler_params=_params(("parallel",)),
    )(o_a, o_b, g_a, g_b)


def _attn_a_kernel(q_ref, k_ref, v_ref, bias_ref, o_ref, ml_ref, *, seq_len):
    t = seq_len
    lane = lax.broadcasted_iota(jnp.int32, (TQ, LANES), 1)
    max_lanes = lane < (LANES // 2)

    def init(i, c):
        rows = pl.ds(pl.multiple_of(i * TQ, TQ), TQ)
        o_ref[0, 0, rows, :] = jnp.zeros((TQ, LANES), F32)
        ml_ref[rows, :] = jnp.where(max_lanes, NEG, 0.0)
        return c

    lax.fori_loop(0, t // TQ, init, 0)

    for p_idx, (_, dil) in enumerate(DILATED_PATTERNS):
        sub_len = t // dil
        nb = sub_len // TQ

        def rows_of(start, size, dil=dil):
            if dil == 1:
                return pl.ds(start, size)
            return pl.ds(start, size, stride=dil)

        def group(it, c, p_idx=p_idx, dil=dil, sub_len=sub_len, nb=nb, rows_of=rows_of):
            loaded = []
            for u in range(UNROLL_A):
                idx = it * UNROLL_A + u
                r = idx // nb
                n = idx % nb
                band_start = jnp.clip(n * TQ - HALF, 0, sub_len - BAND)
                variant = jnp.where(n == 0, 0, jnp.where(n == nb - 1, 2, 1))
                q_rows = rows_of(r + dil * (n * TQ), TQ)
                k_rows = rows_of(r + dil * band_start, BAND)
                q = q_ref[0, 0, q_rows, :].astype(BF16)
                kb = k_ref[0, 0, k_rows, :].astype(BF16)
                vb = v_ref[0, 0, k_rows, :].astype(BF16)
                bias = bias_ref[0, p_idx, variant]
                o_old = o_ref[0, 0, q_rows, :]
                ml_old = ml_ref[q_rows, :]
                loaded.append((q_rows, q, kb, vb, bias, o_old, ml_old))
            results = []
            for q_rows, q, kb, vb, bias, o_old, ml_old in loaded:
                s = lax.dot_general(q, kb, _NT, preferred_element_type=F32) * SCALE + bias
                m_old = ml_old[:, 0:1]
                l_old = ml_old[:, LANES // 2:LANES // 2 + 1]
                m_new = jnp.maximum(m_old, jnp.max(s, axis=-1, keepdims=True))
                alpha = jnp.exp(m_old - m_new)
                p = jnp.exp(s - m_new)
                l_new = alpha * l_old + jnp.sum(p, axis=-1, keepdims=True)
                o_new = alpha * o_old + jnp.dot(p.astype(BF16), vb, preferred_element_type=F32)
                results.append((q_rows, o_new, jnp.where(max_lanes, m_new, l_new)))
            for q_rows, o_new, ml_new in results:
                o_ref[0, 0, q_rows, :] = o_new
                ml_ref[q_rows, :] = ml_new
            return c

        lax.fori_loop(0, (dil * nb) // UNROLL_A, group, 0)

    def finish(i, c):
        rows = pl.ds(pl.multiple_of(i * TQ, TQ), TQ)
        l = ml_ref[rows, :][:, LANES // 2:LANES // 2 + 1]
        o_ref[0, 0, rows, :] = o_ref[0, 0, rows, :] / l
        return c

    lax.fori_loop(0, t // TQ, finish, 0)


def _dilated_attention(qkv_a, bias_a, n_heads):
    n_seq, _, t, _ = qkv_a.shape
    for _, dil in DILATED_PATTERNS:
        assert t % (dil * TQ) == 0 and (t // dil) >= BAND
        assert (t // TQ) % UNROLL_A == 0

    def slab(offset):
        return pl.BlockSpec((1, 1, t, LANES), lambda b, h: (b, offset + h, 0, 0))

    return pl.pallas_call(
        functools.partial(_attn_a_kernel, seq_len=t),
        grid=(n_seq, n_heads),
        in_specs=[slab(0), slab(n_heads), slab(2 * n_heads),
                  pl.BlockSpec((1,) + bias_a.shape[1:], lambda b, h: (h, 0, 0, 0, 0))],
        out_specs=pl.BlockSpec((1, 1, t, LANES), lambda b, h: (b, h, 0, 0)),
        out_shape=jax.ShapeDtypeStruct((n_seq, n_heads, t, LANES), F32),
        scratch_shapes=[pltpu.VMEM((t, LANES), F32)],
        compiler_params=_params(("parallel", "arbitrary")),
    )(qkv_a, qkv_a, qkv_a, bias_a)


def _t5_bucket(rel):
    nb = N_BUCKETS // 2
    ret = jnp.where(rel > 0, nb, 0)
    n = jnp.abs(rel)
    max_exact = nb // 2
    n_f = jnp.maximum(n, 1).astype(F32)
    large = max_exact + (jnp.log(n_f / max_exact) / math.log(MAX_DISTANCE / max_exact)
                         * (nb - max_exact)).astype(jnp.int32)
    large = jnp.minimum(large, nb - 1)
    return ret + jnp.where(n < max_exact, n, large)


def _dilated_bias_table(rel_bias):
    i = jnp.arange(TQ)[:, None]
    j = jnp.arange(BAND)[None, :]
    per_branch = []
    for _, dil in DILATED_PATTERNS:
        variants = []
        for offset in (0, HALF, 2 * HALF):
            rel = j - offset - i
            bias = rel_bias[_t5_bucket(rel * dil)].astype(F32)
            bias = jnp.where((jnp.abs(rel) <= HALF)[..., None], bias, NEG)
            variants.append(jnp.transpose(bias, (2, 0, 1)))
        per_branch.append(jnp.stack(variants, axis=1))
    return jnp.stack(per_branch, axis=1)


def _attn_b_kernel(q_ref, k_ref, v_ref, bias_ref, o_ref, *, rows):
    win = NA_ROWS * GRID_W

    def group(it, c):
        for u in range(UNROLL_B):
            r = it * UNROLL_B + u
            r0 = jnp.clip(r - NA_ROWS // 2, 0, rows - NA_ROWS)
            q = q_ref[0, 0, pl.ds(pl.multiple_of(r * GRID_W, GRID_W), GRID_W), :]
            k_rows = pl.ds(pl.multiple_of(r0 * GRID_W, GRID_W), win)
            kw = k_ref[0, 0, k_rows, :]
            vw = v_ref[0, 0, k_rows, :]
            s = lax.dot_general(q, kw, _NT, preferred_element_type=F32) * SCALE + bias_ref[0, r - r0]
            m = jnp.max(s, axis=-1, keepdims=True)
            p = jnp.exp(s - m)
            l = jnp.sum(p, axis=-1, keepdims=True)
            o = jnp.dot(p.astype(BF16), vw, preferred_element_type=F32) / l
            o_ref[0, 0, pl.ds(pl.multiple_of(r * GRID_W, GRID_W), GRID_W), :] = o
        return c

    lax.fori_loop(0, rows // UNROLL_B, group, 0)


def _neighbourhood_attention(qkv_b, bias_b, n_heads):
    n_seq, _, t, _ = qkv_b.shape
    rows = t // GRID_W
    assert rows >= NA_ROWS and rows % UNROLL_B == 0

    def slab(offset):
        return pl.BlockSpec((1, 1, t, LANES), lambda b, h: (b, offset + h, 0, 0))

    return pl.pallas_call(
        functools.partial(_attn_b_kernel, rows=rows),
        grid=(n_seq, n_heads),
        in_specs=[slab(0), slab(n_heads), slab(2 * n_heads),
                  pl.BlockSpec((1,) + bias_b.shape[1:], lambda b, h: (h, 0, 0, 0))],
        out_specs=pl.BlockSpec((1, 1, t, LANES), lambda b, h: (b, h, 0, 0)),
        out_shape=jax.ShapeDtypeStruct((n_seq, n_heads, t, LANES), F32),
        compiler_params=_params(("parallel", "arbitrary")),
    )(qkv_b, qkv_b, qkv_b, bias_b)


def _neighbourhood_bias_table(rpb):
    a = jnp.arange(NA_ROWS)[:, None, None, None]
    c = jnp.arange(GRID_W)[None, :, None, None]
    kr = jnp.arange(NA_ROWS)[None, None, :, None]
    kc = jnp.arange(GRID_W)[None, None, None, :]
    c0 = jnp.clip(c - NA_COLS // 2, 0, GRID_W - NA_COLS)
    col_ok = (kc >= c0) & (kc < c0 + NA_COLS)
    dr = kr - a
    dc = jnp.clip(kc - c, -(NA_COLS - 1), NA_COLS - 1)
    bias = rpb.astype(F32)[:, dr + NA_ROWS - 1, dc + NA_COLS - 1]
    bias = jnp.where(col_ok[None], bias, NEG)
    return bias.reshape(rpb.shape[0], NA_ROWS, GRID_W, NA_ROWS * GRID_W)


def _encoder_layer(x32, x16, n_seq, alpha, w_in, w_out, b_out, g_a, g_b, bias_a, rpb,
                   ln1_g, ln1_b, w1, b1, w2, b2, ln2_g, ln2_b):
    d_model = x32.shape[1]
    n_heads_a = g_a.shape[0] // HEAD_DIM
    n_heads_b = g_b.shape[0] // HEAD_DIM
    width_a = 3 * n_heads_a * HEAD_DIM
    row = lambda v: v.reshape(1, -1).astype(F32)

    w_in16 = w_in.astype(BF16)
    qkv_a = _matmul_heads(x16, w_in16[:, :width_a], n_seq, F32)
    qkv_b = _matmul_heads(x16, w_in16[:, width_a:], n_seq, BF16)
    o_a = _dilated_attention(qkv_a, bias_a, n_heads_a)
    o_b = _neighbourhood_attention(qkv_b, _neighbourhood_bias_table(rpb), n_heads_b)
    mix = _branch_norm_concat(o_a, o_b, row(g_a), row(g_b))
    z = _matmul_resid(mix, w_out.astype(BF16), row(b_out), x32, alpha, tk=d_model)
    x32, x16 = _layer_norm(z, row(ln1_g), row(ln1_b))
    h = _matmul_relu2(x16, w1.astype(BF16), row(b1))
    z = _matmul_resid(h, w2.astype(BF16), row(b2), x32, alpha)
    return _layer_norm(z, row(ln2_g), row(ln2_b))


def kernel(x_prompt, x_sample, w_in, w_out, b_out, g_a, g_b, rel_bias, rpb, ln1_g, ln1_b, w1, b1, w2, b2, ln2_g, ln2_b):
    assert x_prompt.shape[1:] == x_sample.shape[1:]
    depth = w_in.shape[0]
    alpha = (2 * depth) ** 0.25
    n_prompt = x_prompt.shape[0]
    x = jnp.concatenate([x_prompt, x_sample], axis=0)
    n_seq, t, d_model = x.shape
    x32 = x.reshape(n_seq * t, d_model)
    x16 = x32.astype(BF16)
    bias_a = _dilated_bias_table(rel_bias)
    for l in range(depth):
        x32, x16 = _encoder_layer(x32, x16, n_seq, alpha, w_in[l], w_out[l], b_out[l], g_a[l], g_b[l],
                                  bias_a, rpb[l], ln1_g[l], ln1_b[l], w1[l], b1[l], w2[l], b2[l],
                                  ln2_g[l], ln2_b[l])
    y = x32.reshape(n_seq, t, d_model)
    return (y[:n_prompt], y[n_prompt:])
```
